```python
import math
import jax, jax.numpy as jnp
from jax import lax
import numpy as np


D_MODEL = 2048
BATCH = 1
SEQ = 8192
DEPTH = 1
DEC_BATCH = 32
DEC_SEQ = 4
PAST_LEN = 8192
PAGE_SIZE = 128

HEAD_DIM = 128
MIX_WIDTH = D_MODEL
SB_WIDTH = MIX_WIDTH // 2
SB_HEADS = SB_WIDTH // HEAD_DIM
LRU_WIDTH = MIX_WIDTH - SB_WIDTH
LRU_BLOCKS = 8
LRU_BLOCK_W = LRU_WIDTH // LRU_BLOCKS
CONV_W = 4
LRU_C = 8.0
N_MEM = 256
MEM_HEADS = 4
MEM_WIDTH = MEM_HEADS * HEAD_DIM
D_FF = 4 * D_MODEL
Q_BLOCK = 128
EPS = 1e-6
SB_BIAS_INIT = -7.0
IN_WIDTH = 3 * SB_WIDTH + 2 * LRU_WIDTH
SCALE = 1.0 / math.sqrt(HEAD_DIM)

kernel_name = 'hymba_stickbreak_rglru_mem_decoder_step'


def rms_norm(x, g):
    xf = x.astype(jnp.float32)
    y = xf * lax.rsqrt(jnp.mean(xf * xf, axis=-1, keepdims=True) + EPS)
    return (y * g.astype(jnp.float32)).astype(x.dtype)


def mixer_projections(x, g, w_in):
    b, t, _ = x.shape
    proj = rms_norm(x, g) @ w_in
    q, k, v, xb, gb = jnp.split(proj, [SB_WIDTH, 2 * SB_WIDTH, 3 * SB_WIDTH, 3 * SB_WIDTH + LRU_WIDTH], axis=-1)
    hs = (b, t, SB_HEADS, HEAD_DIM)
    return q.reshape(hs), k.reshape(hs), v.reshape(hs), xb, gb


def sb_block(q, k, v, bias, q_pos, k_pos):
    z = jnp.einsum('bqhd,bkhd->bhqk', q, k).astype(jnp.float32) * SCALE + bias.astype(jnp.float32)[None, :, None, None]
    mask = k_pos[None, :] < q_pos[:, None]
    log_keep = jnp.where(mask, jax.nn.log_sigmoid(-z), 0.0)
    between = lax.cumsum(log_keep, axis=3, reverse=True) - log_keep
    w = jnp.where(mask, jnp.exp(jax.nn.log_sigmoid(z) + between), 0.0)
    return jnp.einsum('bhqk,bkhd->bqhd', w.astype(v.dtype), v)


def sb_attention_prompt(q, k, v, bias):
    b, t, h, d = q.shape
    nb = t // Q_BLOCK
    q_blocks = q.reshape(b, nb, Q_BLOCK, h, d).transpose(1, 0, 2, 3, 4)
    pos = jnp.arange(t, dtype=jnp.int32)
    pos_blocks = pos.reshape(nb, Q_BLOCK)
    out = lax.map(lambda qp: sb_block(qp[0], k, v, bias, qp[1], pos), (q_blocks, pos_blocks))
    return out.transpose(1, 0, 2, 3, 4).reshape(b, t, h, d)


def causal_conv(x, buf, w, bias):
    t = x.shape[1]
    xp = jnp.concatenate([buf.astype(x.dtype), x], axis=1)
    y = bias
    for i in range(CONV_W):
        y = y + w[i] * xp[:, i:i + t]
    return y, xp[:, -(CONV_W - 1):]


def rg_lru(x, h0, w_a, b_a, w_x, b_x, lam):
    b, t, c = x.shape
    xf = x.astype(jnp.float32)
    xg = xf.reshape(b, t, LRU_BLOCKS, LRU_BLOCK_W)
    r = jax.nn.sigmoid(jnp.einsum('btnc,ncd->btnd', xg, w_a.astype(jnp.float32)).reshape(b, t, c) + b_a.astype(jnp.float32))
    i = jax.nn.sigmoid(jnp.einsum('btnc,ncd->btnd', xg, w_x.astype(jnp.float32)).reshape(b, t, c) + b_x.astype(jnp.float32))
    log_a = -LRU_C * r * jax.nn.softplus(-lam.astype(jnp.float32))
    a = jnp.exp(log_a)
    u = jnp.sqrt(-jnp.expm1(2.0 * log_a)) * i * xf

    def step(h, au):
        h = au[0] * h + au[1]
        return h, h

    h_last, hs = lax.scan(step, h0.astype(jnp.float32), (jnp.swapaxes(a, 0, 1), jnp.swapaxes(u, 0, 1)))
    return jnp.swapaxes(hs, 0, 1).astype(x.dtype), h_last.astype(h0.dtype)


def recurrent_branch(xb, gb, conv_buf, h0, conv_w, conv_b, w_a, b_a, w_x, b_x, lam):
    xc, new_buf = causal_conv(xb, conv_buf, conv_w, conv_b)
    y, h_last = rg_lru(xc, h0, w_a, b_a, w_x, b_x, lam)
    return y * jax.nn.gelu(gb), new_buf, h_last


def mem_kv(mem, g, w_k, w_v, k_gain):
    b, n, _ = mem.shape
    m = rms_norm(mem, g)
    k = rms_norm((m @ w_k).reshape(b, n, MEM_HEADS, HEAD_DIM), k_gain)
    v = (m @ w_v).reshape(b, n, MEM_HEADS, HEAD_DIM)
    return k, v


def mem_attend(xn, w_q, q_gain, mk, mv, w_o):
    b, t, _ = xn.shape
    q = rms_norm((xn @ w_q).reshape(b, t, MEM_HEADS, HEAD_DIM), q_gain)
    s = jnp.einsum('bthd,bnhd->bhtn', q, mk.astype(q.dtype)).astype(jnp.float32) * SCALE
    p = jax.nn.softmax(s, axis=-1)
    o = jnp.einsum('bhtn,bnhd->bthd', p.astype(mv.dtype), mv).reshape(b, t, MEM_WIDTH)
    return o.astype(xn.dtype) @ w_o


def finish_layer(x, sb_out, lru_out, w_out, mk, mv, norm_mem_q, w_mq, mq_gain, w_mo, norm_ffn, w_up, w_down):
    b, t, _ = x.shape
    mix = jnp.concatenate([sb_out.reshape(b, t, SB_WIDTH).astype(x.dtype), lru_out.astype(x.dtype)], axis=-1)
    h = x + mix @ w_out
    h = h + mem_attend(rms_norm(h, norm_mem_q), w_mq, mq_gain, mk, mv, w_mo)
    f = jnp.square(jax.nn.relu(rms_norm(h, norm_ffn) @ w_up))
    return h + f @ w_down


def setup_inputs(seed: int = 0) -> dict:
    key = jax.random.key(seed)
    ks = jax.random.split(key, 40)
    f32 = jnp.float32

    def nrm(k, shape, scale=1.0):
        return jax.random.normal(k, shape, f32) * scale

    def gain(k, shape):
        return 1.0 + 0.02 * jax.random.normal(k, shape, f32)

    n_pages = PAST_LEN // PAGE_SIZE
    n_used = DEC_BATCH * n_pages
    n_pool = n_used + max(1, n_used // 4)

    u = jax.random.uniform(ks[20], (DEPTH, LRU_WIDTH), f32, minval=0.9, maxval=0.999)
    a0 = u ** (1.0 / LRU_C)
    lru_lambda = jnp.log(a0) - jnp.log1p(-a0)

    return {
        'x_prompt': nrm(ks[0], (BATCH, SEQ, D_MODEL)),
        'x_sample': nrm(ks[1], (DEC_BATCH, DEC_SEQ, D_MODEL)),
        'mem_prompt': nrm(ks[2], (BATCH, N_MEM, D_MODEL)),
        'cache_sb_k': nrm(ks[3], (DEPTH, n_pool, PAGE_SIZE, SB_HEADS, HEAD_DIM)),
        'cache_sb_v': nrm(ks[4], (DEPTH, n_pool, PAGE_SIZE, SB_HEADS, HEAD_DIM)),
        'page_table': jax.random.permutation(ks[5], n_pool)[:n_used].reshape(DEC_BATCH, n_pages).astype(jnp.int32),
        'state_conv': nrm(ks[6], (DEPTH, DEC_BATCH, CONV_W - 1, LRU_WIDTH)),
        'state_h': nrm(ks[7], (DEPTH, DEC_BATCH, LRU_WIDTH), 0.5),
        'cache_mem_k': nrm(ks[8], (DEPTH, DEC_BATCH, N_MEM, MEM_HEADS, HEAD_DIM)),
        'cache_mem_v': nrm(ks[9], (DEPTH, DEC_BATCH, N_MEM, MEM_HEADS, HEAD_DIM)),
        'norm_mix': gain(ks[10], (DEPTH, D_MODEL)),
        'w_in': nrm(ks[11], (DEPTH, D_MODEL, IN_WIDTH), D_MODEL ** -0.5),
        'sb_bias': SB_BIAS_INIT + 0.1 * jax.random.normal(ks[19], (DEPTH, SB_HEADS), f32),
        'conv_w': nrm(ks[12], (DEPTH, CONV_W, LRU_WIDTH), CONV_W ** -0.5),
        'conv_b': nrm(ks[13], (DEPTH, LRU_WIDTH), 0.01),
        'gate_a_w': nrm(ks[14], (DEPTH, LRU_BLOCKS, LRU_BLOCK_W, LRU_BLOCK_W), LRU_BLOCK_W ** -0.5),
        'gate_a_b': nrm(ks[15], (DEPTH, LRU_WIDTH), 0.01),
        'gate_x_w': nrm(ks[16], (DEPTH, LRU_BLOCKS, LRU_BLOCK_W, LRU_BLOCK_W), LRU_BLOCK_W ** -0.5),
        'gate_x_b': nrm(ks[17], (DEPTH, LRU_WIDTH), 0.01),
        'lru_lambda': lru_lambda,
        'w_out': nrm(ks[18], (DEPTH, MIX_WIDTH, D_MODEL), MIX_WIDTH ** -0.5),
        'norm_mem_q': gain(ks[21], (DEPTH, D_MODEL)),
        'norm_mem_kv': gain(ks[22], (DEPTH, D_MODEL)),
        'w_mq': nrm(ks[23], (DEPTH, D_MODEL, MEM_WIDTH), D_MODEL ** -0.5),
        'w_mk': nrm(ks[24], (DEPTH, D_MODEL, MEM_WIDTH), D_MODEL ** -0.5),
        'w_mv': nrm(ks[25], (DEPTH, D_MODEL, MEM_WIDTH), D_MODEL ** -0.5),
        'mq_gain': gain(ks[26], (DEPTH, HEAD_DIM)),
        'mk_gain': gain(ks[27], (DEPTH, HEAD_DIM)),
        'w_mo': nrm(ks[28], (DEPTH, MEM_WIDTH, D_MODEL), MEM_WIDTH ** -0.5),
        'norm_ffn': gain(ks[29], (DEPTH, D_MODEL)),
        'w_up': nrm(ks[30], (DEPTH, D_MODEL, D_FF), D_MODEL ** -0.5),
        'w_down': nrm(ks[31], (DEPTH, D_FF, D_MODEL), D_FF ** -0.5),
    }


def reference(x_prompt, x_sample, mem_prompt, cache_sb_k, cache_sb_v, page_table, state_conv, state_h,
              cache_mem_k, cache_mem_v, norm_mix, w_in, sb_bias, conv_w, conv_b, gate_a_w, gate_a_b, gate_x_w, gate_x_b,
              lru_lambda, w_out, norm_mem_q, norm_mem_kv, w_mq, w_mk, w_mv, mq_gain, mk_gain, w_mo,
              norm_ffn, w_up, w_down):
    xp = x_prompt
    xs = x_sample
    bp = xp.shape[0]
    db, ts, _ = xs.shape
    past = page_table.shape[1] * PAGE_SIZE
    q_pos_s = past + jnp.arange(ts, dtype=jnp.int32)
    k_pos_s = jnp.arange(past + ts, dtype=jnp.int32)

    pk, pv, sk, sv, pconv, ph, sconv, sh, pmk, pmv = ([] for _ in range(10))
    for l in range(DEPTH):
        lru_params = (conv_w[l], conv_b[l], gate_a_w[l], gate_a_b[l], gate_x_w[l], gate_x_b[l], lru_lambda[l])
        tail_params = (norm_mem_q[l], w_mq[l], mq_gain[l], w_mo[l], norm_ffn[l], w_up[l], w_down[l])

        q, k, v, xb, gb = mixer_projections(xp, norm_mix[l], w_in[l])
        sb_p = sb_attention_prompt(q, k, v, sb_bias[l])
        buf0 = jnp.zeros((bp, CONV_W - 1, LRU_WIDTH), xb.dtype)
        h0 = jnp.zeros((bp, LRU_WIDTH), jnp.float32)
        lru_p, buf_p, h_p = recurrent_branch(xb, gb, buf0, h0, *lru_params)
        mk_p, mv_p = mem_kv(mem_prompt, norm_mem_kv[l], w_mk[l], w_mv[l], mk_gain[l])
        xp = finish_layer(xp, sb_p, lru_p, w_out[l], mk_p, mv_p, *tail_params)
        pk.append(k); pv.append(v); pconv.append(buf_p); ph.append(h_p); pmk.append(mk_p); pmv.append(mv_p)

        q, k, v, xb, gb = mixer_projections(xs, norm_mix[l], w_in[l])
        k_past = cache_sb_k[l][page_table].reshape(db, past, SB_HEADS, HEAD_DIM)
        v_past = cache_sb_v[l][page_table].reshape(db, past, SB_HEADS, HEAD_DIM)
        k_all = jnp.concatenate([k_past, k.astype(k_past.dtype)], axis=1)
        v_all = jnp.concatenate([v_past, v.astype(v_past.dtype)], axis=1)
        sb_s = sb_block(q.astype(k_all.dtype), k_all, v_all, sb_bias[l], q_pos_s, k_pos_s)
        lru_s, buf_s, h_s = recurrent_branch(xb, gb, state_conv[l], state_h[l], *lru_params)
        xs = finish_layer(xs, sb_s, lru_s, w_out[l], cache_mem_k[l], cache_mem_v[l], *tail_params)
        sk.append(k); sv.append(v); sconv.append(buf_s); sh.append(h_s)

    new_sb_k_prompt = jnp.stack(pk)
    new_sb_v_prompt = jnp.stack(pv)
    new_sb_k_sample = jnp.stack(sk)
    new_sb_v_sample = jnp.stack(sv)
    conv_prompt = jnp.stack(pconv)
    h_prompt = jnp.stack(ph)
    conv_sample = jnp.stack(sconv)
    h_sample = jnp.stack(sh)
    mem_k_prompt = jnp.stack(pmk)
    mem_v_prompt = jnp.stack(pmv)
    return (xp, xs, new_sb_k_prompt, new_sb_v_prompt, new_sb_k_sample, new_sb_v_sample,
            conv_prompt, h_prompt, conv_sample, h_sample, mem_k_prompt, mem_v_prompt)
```

```python
import functools
import math

import jax
import jax.numpy as jnp
from jax import lax
from jax.experimental import pallas as pl
from jax.experimental.pallas import tpu as pltpu

F32 = jnp.float32
BF16 = jnp.bfloat16

HEAD_DIM = 128
PAGE_SIZE = 128
LRU_BLOCKS = 8
LRU_BLOCK_W = 128
CONV_W = 4
LRU_C = 8.0
MEM_HEADS = 4
EPS = 1e-6
SCALE = 1.0 / math.sqrt(HEAD_DIM)

VMEM_LIMIT_BYTES = 56 * 1024 * 1024

SB_TQ = 256
SB_TK = 256
PAGES_PER_STEP = 4
LRU_TT = 256


def _params(*semantics):
    return pltpu.CompilerParams(dimension_semantics=semantics, vmem_limit_bytes=VMEM_LIMIT_BYTES)


def _rms(x, g):
    return x * lax.rsqrt(jnp.mean(x * x, axis=-1, keepdims=True) + EPS) * g


def _softplus(z):
    return jnp.maximum(z, 0.0) + jnp.log1p(jnp.exp(-jnp.abs(z)))


def _dot(a, b):
    return jnp.dot(a, b, preferred_element_type=F32)


def _dot_nt(a, b):
    return lax.dot_general(a, b, (((1,), (1,)), ((), ())), preferred_element_type=F32)


def _split_bf16(x):
    hi = x.astype(BF16)
    lo = (x - hi.astype(F32)).astype(BF16)
    return hi, lo


def _suffix_tri(n):
    r = lax.broadcasted_iota(jnp.int32, (n, n), 0)
    c = lax.broadcasted_iota(jnp.int32, (n, n), 1)
    return (r >= c).astype(BF16)


def _inproj_kernel(x_ref, g_ref, w_ref, q_ref, k_ref, v_ref, xb_ref, gb_ref, xn_ref):
    j = pl.program_id(1)

    @pl.when(j == 0)
    def _():
        xn_ref[...] = _rms(x_ref[...], g_ref[...]).astype(BF16)

    r = _dot(xn_ref[...], w_ref[...])
    for c, o_ref in enumerate((q_ref, k_ref, v_ref, xb_ref, gb_ref)):
        @pl.when(j == c)
        def _(o_ref=o_ref):
            o_ref[...] = r


def _inproj(x, g, w, tm):
    rows, d = x.shape
    n = w.shape[1]
    tn = n // 5
    out = jax.ShapeDtypeStruct((rows, tn), F32)
    out_spec = pl.BlockSpec((tm, tn), lambda i, j: (i, 0))
    return pl.pallas_call(
        _inproj_kernel,
        grid=(rows // tm, 5),
        in_specs=[
            pl.BlockSpec((tm, d), lambda i, j: (i, 0)),
            pl.BlockSpec((1, d), lambda i, j: (0, 0)),
            pl.BlockSpec((d, tn), lambda i, j: (0, j)),
        ],
        out_specs=[out_spec] * 5,
        out_shape=[out] * 5,
        scratch_shapes=[pltpu.VMEM((tm, d), BF16)],
        compiler_params=_params("parallel", "arbitrary"),
        name="inproj",
    )(x, g, w)


def _sb_block(qs, kblk, vblk, tri, bias, run, mask):
    z = _dot_nt(qs, kblk) + bias
    lk = -_softplus(z)
    if mask is not None:
        lk = jnp.where(mask, lk, 0.0)
    hi, lo = _split_bf16(lk)
    c = _dot(hi, tri) + _dot(lo, tri)
    w = jnp.exp(z + c + run)
    if mask is not None:
        w = jnp.where(mask, w, 0.0)
    return _dot(w.astype(BF16), vblk), run + c[:, 0:1]


def _sb_prompt_kernel(bias_ref, q_ref, k_ref, v_ref, tri_ref, o_ref, kb_ref, vb_ref):
    h = pl.program_id(0)
    i = pl.program_id(1)

    @pl.when(i == 0)
    def _():
        kb_ref[...] = k_ref[...].astype(BF16)
        vb_ref[...] = v_ref[...].astype(BF16)

    bias = bias_ref[h]
    qs = (q_ref[...] * SCALE).astype(BF16)
    tri = tri_ref[...]

    def kv(j):
        start = pl.multiple_of(j * SB_TK, SB_TK)
        return kb_ref[pl.ds(start, SB_TK), :], vb_ref[pl.ds(start, SB_TK), :]

    row = lax.broadcasted_iota(jnp.int32, (SB_TQ, SB_TK), 0)
    col = lax.broadcasted_iota(jnp.int32, (SB_TQ, SB_TK), 1)
    kblk, vblk = kv(i)
    acc, run = _sb_block(qs, kblk, vblk, tri, bias, jnp.zeros((SB_TQ, 1), F32), col < row)

    def body(jj, carry):
        acc, run = carry
        kblk, vblk = kv(i - 1 - jj)
        pv, run = _sb_block(qs, kblk, vblk, tri, bias, run, None)
        return acc + pv, run

    acc, _ = lax.fori_loop(0, i, body, (acc, run))
    o_ref[...] = acc


def _sb_prompt(q, k, v, bias):
    t, width = q.shape
    heads = width // HEAD_DIM
    kv_spec = pl.BlockSpec((t, HEAD_DIM), lambda h, i: (0, h))
    q_spec = pl.BlockSpec((SB_TQ, HEAD_DIM), lambda h, i: (i, h))
    return pl.pallas_call(
        _sb_prompt_kernel,
        grid=(heads, t // SB_TQ),
        in_specs=[
            pl.BlockSpec(memory_space=pltpu.SMEM),
            q_spec, kv_spec, kv_spec,
            pl.BlockSpec((SB_TK, SB_TK), lambda h, i: (0, 0)),
        ],
        out_specs=q_spec,
        out_shape=jax.ShapeDtypeStruct((t, width), F32),
        scratch_shapes=[pltpu.VMEM((t, HEAD_DIM), BF16), pltpu.VMEM((t, HEAD_DIM), BF16)],
        compiler_params=_params("parallel", "arbitrary"),
        name="sb_prompt",
    )(bias, q, k, v, _suffix_tri(SB_TK))


def _sb_sample_kernel(pt_ref, qbd_ref, bias_ref, kn_ref, vn_ref, tri_ref, *rest):
    g_pages = PAGES_PER_STEP
    k_refs = rest[:g_pages]
    v_refs = rest[g_pages:2 * g_pages]
    o_ref, acc_ref, run_ref = rest[2 * g_pages:]
    del pt_ref
    s = pl.program_id(1)
    qf = qbd_ref[...]
    rows, width = qf.shape
    heads = width // HEAD_DIM

    @pl.when(s == 0)
    def _():
        kn = kn_ref[...]
        vn = vn_ref[...]
        ts = kn.shape[0]
        row_t = lax.broadcasted_iota(jnp.int32, (rows, 1), 0) // heads
        bias_col = bias_ref[:, 0:1]
        zs = [jnp.sum(qf * kn[j:j + 1, :], axis=1, keepdims=True) * SCALE + bias_col for j in range(ts)]
        suffix = jnp.zeros((rows, 1), F32)
        acc = jnp.zeros((rows, width), F32)
        for j in reversed(range(ts)):
            valid = row_t > j
            suffix = suffix + jnp.where(valid, -_softplus(zs[j]), 0.0)
            wj = jnp.where(valid, jnp.exp(zs[j] + suffix), 0.0)
            acc = acc + wj * vn[j:j + 1, :]
        acc_ref[...] = acc
        run_ref[...] = jnp.broadcast_to(suffix, run_ref.shape)

    qb = qf.astype(BF16)
    sc = jnp.concatenate([_dot_nt(qb, k_ref[...].astype(BF16)) for k_ref in k_refs], axis=1)
    z = sc * SCALE + bias_ref[...]
    lk = -_softplus(z)
    hi, lo = _split_bf16(lk)
    tri = tri_ref[...]
    c = _dot(hi, tri) + _dot(lo, tri)
    run = run_ref[...]
    w = jnp.exp(z + c + jnp.concatenate([run] * g_pages, axis=1))
    vcat = jnp.concatenate([v_ref[...].astype(BF16) for v_ref in v_refs], axis=0)
    acc_ref[...] += _dot(w.astype(BF16), vcat)
    run_ref[...] = run + jnp.broadcast_to(c[:, 0:1], run.shape)

    @pl.when(s == pl.num_programs(1) - 1)
    def _():
        acc = acc_ref[...]
        head_of_lane = lax.broadcasted_iota(jnp.int32, (heads, width), 1) // HEAD_DIM
        head_of_row = lax.broadcasted_iota(jnp.int32, (heads, width), 0)
        sel = (head_of_lane == head_of_row).astype(F32)
        o_ref[...] = jnp.sum(acc.reshape(rows // heads, heads, width) * sel[None], axis=1)


def _sb_sample(q, k_new, v_new, cache_k, cache_v, page_table, bias):
    db, ts, width = q.shape
    heads = width // HEAD_DIM
    n_pages = page_table.shape[1]
    g_pages = PAGES_PER_STEP
    steps = n_pages // g_pages
    rows = ts * heads
    span = g_pages * PAGE_SIZE

    q4 = q.reshape(db, ts, heads, HEAD_DIM)
    qbd = jnp.einsum("bthd,hg->bthgd", q4, jnp.eye(heads, dtype=F32)).reshape(db, rows, width)
    bias_rows = jnp.broadcast_to(jnp.tile(bias, ts)[:, None], (rows, span))

    def page_spec(g):
        def index(b, s, pt):
            return (pt[b, n_pages - (s + 1) * g_pages + g], 0, 0)
        return pl.BlockSpec((None, PAGE_SIZE, width), index)

    def batch_spec(r):
        return pl.BlockSpec((None, r, width), lambda b, s, pt: (b, 0, 0))

    grid_spec = pltpu.PrefetchScalarGridSpec(
        num_scalar_prefetch=1,
        grid=(db, steps),
        in_specs=[
            batch_spec(rows),
            pl.BlockSpec((rows, span), lambda b, s, pt: (0, 0)),
            batch_spec(ts), batch_spec(ts),
            pl.BlockSpec((span, span), lambda b, s, pt: (0, 0)),
        ] + [page_spec(g) for g in range(g_pages)] * 2,
        out_specs=batch_spec(ts),
        scratch_shapes=[pltpu.VMEM((rows, width), F32), pltpu.VMEM((rows, HEAD_DIM), F32)],
    )
    return pl.pallas_call(
        _sb_sample_kernel,
        grid_spec=grid_spec,
        out_shape=jax.ShapeDtypeStruct((db, ts, width), F32),
        compiler_params=_params("parallel", "arbitrary"),
        name="sb_sample",
    )(page_table, qbd, bias_rows, k_new, v_new, _suffix_tri(span),
      *([cache_k] * g_pages), *([cache_v] * g_pages))


def _gelu_tanh(x):
    return 0.5 * x * (1.0 + jnp.tanh(math.sqrt(2.0 / math.pi) * (x + 0.044715 * (x * x * x))))


def _lru_gates(xc, wa_ref, ba, wx_ref, bx, lam):
    xg = xc.astype(BF16)
    blocks = [xg[:, n * LRU_BLOCK_W:(n + 1) * LRU_BLOCK_W] for n in range(LRU_BLOCKS)]
    ra = jnp.concatenate([_dot(blocks[n], wa_ref[n]) for n in range(LRU_BLOCKS)], axis=1) + ba
    rx = jnp.concatenate([_dot(blocks[n], wx_ref[n]) for n in range(LRU_BLOCKS)], axis=1) + bx
    r = jax.nn.sigmoid(ra)
    gate_in = jax.nn.sigmoid(rx)
    log_a = -LRU_C * r * _softplus(-lam)
    a = jnp.exp(log_a)
    u = jnp.sqrt(-jnp.tanh(log_a) * (a * a + 1.0)) * gate_in * xc
    return a, u


def _shift_rows(x, s, fill):
    rows = x.shape[0]
    if s % 8 == 0:
        pad = jnp.full((s,) + x.shape[1:], fill, x.dtype)
        return jnp.concatenate([pad, x[:rows - s]], axis=0)
    rolled = pltpu.roll(x, s, 0)
    t = lax.broadcasted_iota(jnp.int32, x.shape, 0)
    return jnp.where(t >= s, rolled, fill)


def _lru_prompt_kernel(xb_ref, gb_ref, cw_ref, cb_ref, wa_ref, ba_ref, wx_ref, bx_ref, lam_ref,
                       y_ref, hlast_ref, xbuf_ref, h_ref):
    i = pl.program_id(0)
    tt = xb_ref.shape[0]
    pad = 8

    @pl.when(i == 0)
    def _():
        xbuf_ref[0:pad, :] = jnp.zeros((pad, xbuf_ref.shape[1]), F32)
        h_ref[...] = jnp.zeros_like(h_ref)

    xbuf_ref[pad:pad + tt, :] = xb_ref[...]
    xc = cb_ref[...]
    for tap in range(CONV_W):
        off = pad - (CONV_W - 1) + tap
        xc = xc + cw_ref[tap:tap + 1, :] * xbuf_ref[off:off + tt, :]
    xbuf_ref[0:pad, :] = xbuf_ref[tt:tt + pad, :]

    a, u = _lru_gates(xc, wa_ref, ba_ref[...], wx_ref, bx_ref[...], lam_ref[...])
    s = 1
    while s < tt:
        u = a * _shift_rows(u, s, 0.0) + u
        a = a * _shift_rows(a, s, 1.0)
        s *= 2
    hs = a * h_ref[...] + u
    h_ref[...] = hs[tt - 1:tt, :]
    y_ref[...] = hs * _gelu_tanh(gb_ref[...])

    @pl.when(i == pl.num_programs(0) - 1)
    def _():
        hlast_ref[...] = hs[tt - 1:tt, :]


def _lru_prompt(xb, gb, conv_w, conv_b, wa, ba, wx, bx, lam):
    t, width = xb.shape
    tt = LRU_TT
    row = pl.BlockSpec((tt, width), lambda i: (i, 0))
    vec = pl.BlockSpec((1, width), lambda i: (0, 0))
    gate = pl.BlockSpec((LRU_BLOCKS, LRU_BLOCK_W, LRU_BLOCK_W), lambda i: (0, 0, 0))
    return pl.pallas_call(
        _lru_prompt_kernel,
        grid=(t // tt,),
        in_specs=[row, row, pl.BlockSpec((CONV_W, width), lambda i: (0, 0)), vec, gate, vec, gate, vec, vec],
        out_specs=[row, vec],
        out_shape=[jax.ShapeDtypeStruct((t, width), F32), jax.ShapeDtypeStruct((1, width), F32)],
        scratch_shapes=[pltpu.VMEM((tt + 8, width), F32), pltpu.VMEM((1, width), F32)],
        compiler_params=_params("arbitrary"),
        name="lru_prompt",
    )(xb, gb, conv_w, conv_b, wa, ba, wx, bx, lam)


def _lru_sample_kernel(xb_ref, gb_ref, sc_ref, h0_ref, cw_ref, cb_ref, wa_ref, ba_ref, wx_ref, bx_ref, lam_ref,
                       y_ref, hlast_ref):
    ts = xb_ref.shape[0]
    xp = [sc_ref[j] for j in range(CONV_W - 1)] + [xb_ref[t] for t in range(ts)]
    xcs = []
    for t in range(ts):
        xc = cb_ref[...]
        for tap in range(CONV_W):
            xc = xc + cw_ref[tap:tap + 1, :] * xp[t + tap]
        xcs.append(xc)
    db = xcs[0].shape[0]
    a, u = _lru_gates(jnp.concatenate(xcs, axis=0), wa_ref, ba_ref[...], wx_ref, bx_ref[...], lam_ref[...])
    h = h0_ref[...]
    for t in range(ts):
        h = a[t * db:(t + 1) * db] * h + u[t * db:(t + 1) * db]
        y_ref[t] = h * _gelu_tanh(gb_ref[t])
    hlast_ref[...] = h


def _lru_sample(xb, gb, state_conv, h0, conv_w, conv_b, wa, ba, wx, bx, lam):
    ts, db, width = xb.shape
    return pl.pallas_call(
        _lru_sample_kernel,
        out_shape=[jax.ShapeDtypeStruct((ts, db, width), F32), jax.ShapeDtypeStruct((db, width), F32)],
        compiler_params=pltpu.CompilerParams(vmem_limit_bytes=VMEM_LIMIT_BYTES),
        name="lru_sample",
    )(xb, gb, state_conv, h0, conv_w, conv_b, wa, ba, wx, bx, lam)


def _head_rms(x, gain):
    heads = x.shape[1] // HEAD_DIM
    return jnp.concatenate(
        [_rms(x[:, h * HEAD_DIM:(h + 1) * HEAD_DIM], gain) for h in range(heads)], axis=1)


def _mem_kv_kernel(mem_ref, g_ref, wk_ref, wv_ref, kg_ref, k_ref, v_ref):
    m = _rms(mem_ref[...], g_ref[...]).astype(BF16)
    k_ref[...] = _head_rms(_dot(m, wk_ref[...]), kg_ref[...])
    v_ref[...] = _dot(m, wv_ref[...])


def _mem_kv(mem, g, wk, wv, k_gain):
    n = mem.shape[0]
    out = jax.ShapeDtypeStruct((n, wk.shape[1]), F32)
    return pl.pallas_call(
        _mem_kv_kernel,
        out_shape=[out, out],
        compiler_params=pltpu.CompilerParams(vmem_limit_bytes=VMEM_LIMIT_BYTES),
        name="mem_kv",
    )(mem, g, wk, wv, k_gain)


def _outproj_kernel(sb_ref, lru_ref, x_ref, wo_ref, g_ref, wq_ref, qg_ref, h_ref, qn_ref):
    half = sb_ref.shape[1]
    h = (x_ref[...] + _dot(sb_ref[...].astype(BF16), wo_ref[0:half, :])
         + _dot(lru_ref[...].astype(BF16), wo_ref[half:2 * half, :]))
    h_ref[...] = h
    q = _dot(_rms(h, g_ref[...]).astype(BF16), wq_ref[...])
    qn_ref[...] = _head_rms(q, qg_ref[...])


def _outproj(sb, lru, x, wo, g, wq, q_gain, tm):
    rows, d = x.shape
    half = sb.shape[1]
    mw = wq.shape[1]
    const = lambda i: (0, 0)
    return pl.pallas_call(
        _outproj_kernel,
        grid=(rows // tm,),
        in_specs=[
            pl.BlockSpec((tm, half), lambda i: (i, 0)),
            pl.BlockSpec((tm, half), lambda i: (i, 0)),
            pl.BlockSpec((tm, d), lambda i: (i, 0)),
            pl.BlockSpec(wo.shape, const),
            pl.BlockSpec((1, d), const),
            pl.BlockSpec(wq.shape, const),
            pl.BlockSpec((1, HEAD_DIM), const),
        ],
        out_specs=[pl.BlockSpec((tm, d), lambda i: (i, 0)), pl.BlockSpec((tm, mw), lambda i: (i, 0))],
        out_shape=[jax.ShapeDtypeStruct((rows, d), F32), jax.ShapeDtypeStruct((rows, mw), F32)],
        compiler_params=_params("parallel"),
        name="outproj",
    )(sb, lru, x, wo, g, wq, q_gain)


def _mem_attn_kernel(q_ref, k_ref, v_ref, o_ref):
    q = q_ref[...]
    k = k_ref[...]
    v = v_ref[...]
    outs = []
    for h in range(MEM_HEADS):
        sl = slice(h * HEAD_DIM, (h + 1) * HEAD_DIM)
        s = _dot_nt(q[:, sl].astype(BF16), k[:, sl].astype(BF16)) * SCALE
        e = jnp.exp(s - jnp.max(s, axis=-1, keepdims=True))
        denom = jnp.sum(e, axis=-1, keepdims=True)
        outs.append(_dot(e.astype(BF16), v[:, sl].astype(BF16)) / denom)
    o_ref[...] = jnp.concatenate(outs, axis=1)


def _mem_attn(q, mk, mv, tq):
    b, t, mw = q.shape
    n = mk.shape[1]
    q_spec = pl.BlockSpec((None, tq, mw), lambda bi, i: (bi, i, 0))
    kv_spec = pl.BlockSpec((None, n, mw), lambda bi, i: (bi, 0, 0))
    return pl.pallas_call(
        _mem_attn_kernel,
        grid=(b, t // tq),
        in_specs=[q_spec, kv_spec, kv_spec],
        out_specs=q_spec,
        out_shape=jax.ShapeDtypeStruct((b, t, mw), F32),
        compiler_params=_params("parallel", "parallel"),
        name="mem_attn",
    )(q, mk, mv)


def _ffn_kernel(h_ref, o_ref, wmo_ref, g_ref, wup_ref, wdn_ref, y_ref, hn_ref):
    f = pl.program_id(1)

    @pl.when(f == 0)
    def _():
        h2 = h_ref[...] + _dot(o_ref[...].astype(BF16), wmo_ref[...])
        y_ref[...] = h2
        hn_ref[...] = _rms(h2, g_ref[...]).astype(BF16)

    up = _dot(hn_ref[...], wup_ref[...])
    act = jnp.square(jnp.maximum(up, 0.0))
    y_ref[...] += _dot(act.astype(BF16), wdn_ref[...])


def _ffn(h, o, wmo, g, wup, wdn, tm, tf):
    rows, d = h.shape
    mw = o.shape[1]
    dff = wup.shape[1]
    return pl.pallas_call(
        _ffn_kernel,
        grid=(rows // tm, dff // tf),
        in_specs=[
            pl.BlockSpec((tm, d), lambda i, f: (i, 0)),
            pl.BlockSpec((tm, mw), lambda i, f: (i, 0)),
            pl.BlockSpec((mw, d), lambda i, f: (0, 0)),
            pl.BlockSpec((1, d), lambda i, f: (0, 0)),
            pl.BlockSpec((d, tf), lambda i, f: (0, f)),
            pl.BlockSpec((tf, d), lambda i, f: (f, 0)),
        ],
        out_specs=pl.BlockSpec((tm, d), lambda i, f: (i, 0)),
        out_shape=jax.ShapeDtypeStruct((rows, d), F32),
        scratch_shapes=[pltpu.VMEM((tm, d), BF16)],
        compiler_params=_params("parallel", "arbitrary"),
        name="ffn",
    )(h, o, wmo, g, wup, wdn)


def kernel(x_prompt, x_sample, mem_prompt, cache_sb_k, cache_sb_v, page_table, state_conv, state_h, cache_mem_k, cache_mem_v, norm_mix, w_in, sb_bias, conv_w, conv_b, gate_a_w, gate_a_b, gate_x_w, gate_x_b, lru_lambda, w_out, norm_mem_q, norm_mem_kv, w_mq, w_mk, w_mv, mq_gain, mk_gain, w_mo, norm_ffn, w_up, w_down):
    depth = w_in.shape[0]
    assert depth == 1 and x_prompt.shape[0] == 1
    bp, seq, d = x_prompt.shape
    db, ts, _ = x_sample.shape
    n_mem = mem_prompt.shape[1]
    l = 0

    row = lambda a: a[l].reshape(1, -1)
    w_in_b = w_in[l].astype(BF16)
    w_out_b = w_out[l].astype(BF16)
    w_mq_b = w_mq[l].astype(BF16)
    w_mk_b = w_mk[l].astype(BF16)
    w_mv_b = w_mv[l].astype(BF16)
    w_mo_b = w_mo[l].astype(BF16)
    w_up_b = w_up[l].astype(BF16)
    w_dn_b = w_down[l].astype(BF16)
    wa_b = gate_a_w[l].astype(BF16)
    wx_b = gate_x_w[l].astype(BF16)
    lru_vecs = (conv_w[l], row(conv_b), wa_b, row(gate_a_b), wx_b, row(gate_x_b), row(lru_lambda))
    bias = sb_bias[l]
    sb_w = w_in.shape[2] // 5
    mem_w = w_mq.shape[2]

    xp = x_prompt.reshape(seq, d)
    q, k_p, v_p, xb_p, gb = _inproj(xp, row(norm_mix), w_in_b, tm=512)
    sb = _sb_prompt(q, k_p, v_p, bias)
    lru, h_p = _lru_prompt(xb_p, gb, *lru_vecs)
    mk_p, mv_p = _mem_kv(mem_prompt.reshape(n_mem, d), row(norm_mem_kv), w_mk_b, w_mv_b, row(mk_gain))
    h, qn = _outproj(sb, lru, xp, w_out_b, row(norm_mem_q), w_mq_b, row(mq_gain), tm=256)
    o = _mem_attn(qn.reshape(1, seq, mem_w), mk_p.reshape(1, n_mem, mem_w), mv_p.reshape(1, n_mem, mem_w), tq=512)
    y_p = _ffn(h, o.reshape(seq, mem_w), w_mo_b, row(norm_ffn), w_up_b, w_dn_b, tm=512, tf=1024)

    rows_s = db * ts
    xs = x_sample.reshape(rows_s, d)
    q, k_s, v_s, xb_s, gb = _inproj(xs, row(norm_mix), w_in_b, tm=rows_s)
    n_pool = cache_sb_k.shape[1]
    sb = _sb_sample(q.reshape(db, ts, sb_w), k_s.reshape(db, ts, sb_w), v_s.reshape(db, ts, sb_w),
                    cache_sb_k[l].reshape(n_pool, PAGE_SIZE, sb_w), cache_sb_v[l].reshape(n_pool, PAGE_SIZE, sb_w),
                    page_table, bias)
    time_major = lambda a: a.reshape(db, ts, -1).transpose(1, 0, 2)
    lru_t, h_s = _lru_sample(time_major(xb_s), time_major(gb), state_conv[l].transpose(1, 0, 2), state_h[l], *lru_vecs)
    lru = lru_t.transpose(1, 0, 2).reshape(rows_s, -1)
    h, qn = _outproj(sb.reshape(rows_s, sb_w), lru, xs, w_out_b, row(norm_mem_q), w_mq_b, row(mq_gain), tm=rows_s)
    tpad = 8
    qn_pad = jnp.pad(qn.reshape(db, ts, mem_w), ((0, 0), (0, tpad - ts), (0, 0)))
    o = _mem_attn(qn_pad, cache_mem_k[l].reshape(db, n_mem, mem_w), cache_mem_v[l].reshape(db, n_mem, mem_w), tq=tpad)
    o = o[:, :ts].reshape(rows_s, mem_w)
    y_s = _ffn(h, o, w_mo_b, row(norm_ffn), w_up_b, w_dn_b, tm=rows_s, tf=1024)

    heads = sb_w // HEAD_DIM
    kv_p = lambda a: a.reshape(1, bp, seq, heads, HEAD_DIM)
    kv_s = lambda a: a.reshape(1, db, ts, heads, HEAD_DIM)
    conv_p = xb_p[seq - (CONV_W - 1):].reshape(1, bp, CONV_W - 1, -1)
    conv_s = xb_s.reshape(db, ts, -1)[:, ts - (CONV_W - 1):].reshape(1, db, CONV_W - 1, -1)
    return (y_p.reshape(bp, seq, d), y_s.reshape(db, ts, d),
            kv_p(k_p), kv_p(v_p), kv_s(k_s), kv_s(v_s),
            conv_p, h_p.reshape(1, bp, -1), conv_s, h_s.reshape(1, db, -1),
            mk_p.reshape(1, bp, n_mem, MEM_HEADS, HEAD_DIM), mv_p.reshape(1, bp, n_mem, MEM_HEADS, HEAD_DIM))
```

```python
import functools
import math

import jax
import jax.numpy as jnp
from jax import lax
from jax.experimental import pallas as pl
from jax.experimental.pallas import tpu as pltpu

F32 = jnp.float32
BF16 = jnp.bfloat16

HEAD_DIM = 128
PAGE_SIZE = 128
LRU_BLOCKS = 8
LRU_BLOCK_W = 128
CONV_W = 4
LRU_C = 8.0
MEM_HEADS = 4
EPS = 1e-6
SCALE = 1.0 / math.sqrt(HEAD_DIM)
LOG2E = math.log2(math.e)

VMEM_LIMIT_BYTES = 56 * 1024 * 1024

SB_TQ = 512
SB_TK = 256
SB_HEADS_PER_STEP = 8
SB_SKEW = 2
PAGES_PER_STEP = 4
SB_SAMPLE_ROWS = 8
LRU_TT = 256


def _params(*semantics):
    return pltpu.CompilerParams(dimension_semantics=semantics, vmem_limit_bytes=VMEM_LIMIT_BYTES)


def _rms(x, g):
    return x * lax.rsqrt(jnp.mean(x * x, axis=-1, keepdims=True) + EPS) * g


def _softplus(z):
    return jnp.maximum(z, 0.0) + jnp.log1p(jnp.exp(-jnp.abs(z)))


def _dot(a, b):
    return jnp.dot(a, b, preferred_element_type=F32)


def _dot_nt(a, b):
    return lax.dot_general(a, b, (((1,), (1,)), ((), ())), preferred_element_type=F32)


def _neg_strict_suffix_tri(n):
    r = lax.broadcasted_iota(jnp.int32, (n, n), 0)
    c = lax.broadcasted_iota(jnp.int32, (n, n), 1)
    return -(r > c).astype(BF16)


def _inproj_kernel(x_ref, g_ref, w_ref, q_ref, k_ref, v_ref, xb_ref, gb_ref, kb_ref, vb_ref, xn_ref):
    j = pl.program_id(1)

    @pl.when(j == 0)
    def _():
        xn_ref[...] = _rms(x_ref[...], g_ref[...]).astype(BF16)

    r = _dot(xn_ref[...], w_ref[...])
    for c, o_ref in enumerate((q_ref, k_ref, v_ref, xb_ref, gb_ref)):
        @pl.when(j == c)
        def _(o_ref=o_ref):
            o_ref[...] = r
    for c, o_ref in ((1, kb_ref), (2, vb_ref)):
        @pl.when(j == c)
        def _(o_ref=o_ref):
            o_ref[...] = r.astype(BF16)


def _inproj(x, g, w, tm):
    rows, d = x.shape
    n = w.shape[1]
    tn = n // 5
    out_spec = pl.BlockSpec((tm, tn), lambda i, j: (i, 0))
    return pl.pallas_call(
        _inproj_kernel,
        grid=(rows // tm, 5),
        in_specs=[
            pl.BlockSpec((tm, d), lambda i, j: (i, 0)),
            pl.BlockSpec((1, d), lambda i, j: (0, 0)),
            pl.BlockSpec((d, tn), lambda i, j: (0, j)),
        ],
        out_specs=[out_spec] * 7,
        out_shape=[jax.ShapeDtypeStruct((rows, tn), F32)] * 5 + [jax.ShapeDtypeStruct((rows, tn), BF16)] * 2,
        scratch_shapes=[pltpu.VMEM((tm, d), BF16)],
        compiler_params=_params("parallel", "arbitrary"),
        name="inproj",
    )(x, g, w)


def _sb_terms(zp):
    pos = jnp.maximum(zp, 0.0)
    neg = jnp.minimum(zp, 0.0)
    t = jnp.log2(1.0 + jnp.exp2(neg - pos))
    return pos + t, neg - t


def _sb_prompt_kernel(bias_ref, q_ref, k_ref, v_ref, tri_ref, o_ref, qs_ref, run_ref):
    hg = pl.program_id(0)
    i = pl.program_id(1)
    n_heads = q_ref.shape[1] // HEAD_DIM
    lanes = [slice(n * HEAD_DIM, (n + 1) * HEAD_DIM) for n in range(n_heads)]
    biases = [bias_ref[hg * n_heads + n] * LOG2E for n in range(n_heads)]
    qs_ref[...] = (q_ref[...] * (SCALE * LOG2E)).astype(BF16)

    def blocks(j, mask, first):
        rows = pl.ds(pl.multiple_of(j * SB_TK, SB_TK), SB_TK)
        zps, ls, cs = [None] * n_heads, [None] * n_heads, [None] * n_heads
        for step in range(n_heads + 2 * SB_SKEW):
            if step < n_heads:
                sl = lanes[step]
                zps[step] = _dot_nt(qs_ref[:, sl], k_ref[rows, sl]) + biases[step]
            h = step - SB_SKEW
            if 0 <= h < n_heads:
                l, zps[h] = _sb_terms(zps[h])
                if mask is not None:
                    l = jnp.where(mask, l, 0.0)
                ls[h] = l[:, 0:1]
                cs[h] = _dot(l.astype(BF16), tri_ref[...])
            h = step - 2 * SB_SKEW
            if 0 <= h < n_heads:
                sl = lanes[h]
                arg = zps[h] + cs[h]
                total = jnp.broadcast_to(cs[h][:, 0:1] - ls[h], (SB_TQ, HEAD_DIM))
                if first:
                    run_ref[:, sl] = total
                else:
                    run = run_ref[:, sl]
                    arg = arg + jnp.concatenate([run] * (SB_TK // HEAD_DIM), axis=1)
                    run_ref[:, sl] = run + total
                w = jnp.exp2(arg)
                if mask is not None:
                    w = jnp.where(mask, w, 0.0)
                pv = _dot(w.astype(BF16), v_ref[rows, sl])
                if first:
                    o_ref[:, sl] = pv
                else:
                    o_ref[:, sl] += pv

    n_diag = SB_TQ // SB_TK
    row = lax.broadcasted_iota(jnp.int32, (SB_TQ, SB_TK), 0)
    col = lax.broadcasted_iota(jnp.int32, (SB_TQ, SB_TK), 1)
    for d in reversed(range(n_diag)):
        blocks(i * n_diag + d, col + d * SB_TK < row, first=(d == n_diag - 1))

    def body(jj, carry):
        blocks(i * n_diag - 1 - jj, None, first=False)
        return carry

    lax.fori_loop(0, i * n_diag, body, 0)


def _sb_prompt(q, kb, vb, bias):
    t, width = q.shape
    group_w = SB_HEADS_PER_STEP * HEAD_DIM
    kv_spec = pl.BlockSpec((t, group_w), lambda h, i: (0, h))
    q_spec = pl.BlockSpec((SB_TQ, group_w), lambda h, i: (i, h))
    return pl.pallas_call(
        _sb_prompt_kernel,
        grid=(width // group_w, t // SB_TQ),
        in_specs=[
            pl.BlockSpec(memory_space=pltpu.SMEM),
            q_spec, kv_spec, kv_spec,
            pl.BlockSpec((SB_TK, SB_TK), lambda h, i: (0, 0)),
        ],
        out_specs=q_spec,
        out_shape=jax.ShapeDtypeStruct((t, width), F32),
        scratch_shapes=[pltpu.VMEM((SB_TQ, group_w), BF16), pltpu.VMEM((SB_TQ, group_w), F32)],
        compiler_params=_params("parallel", "arbitrary"),
        name="sb_prompt",
    )(bias, q, kb, vb, _neg_strict_suffix_tri(SB_TK))


def _sb_sample_kernel(pt_ref, q_ref, bias_ref, kn_ref, vn_ref, tri_ref, *rest):
    g_pages = PAGES_PER_STEP
    k_refs = rest[:g_pages]
    v_refs = rest[g_pages:2 * g_pages]
    o_ref, acc_ref, run_ref = rest[2 * g_pages:]
    del pt_ref
    s = pl.program_id(1)
    qf = q_ref[...] * (SCALE * LOG2E)
    rows = qf.shape[0]
    rpad = SB_SAMPLE_ROWS
    heads = rows // rpad
    ts = kn_ref.shape[0]

    @pl.when(s == 0)
    def _():
        row_t = lax.broadcasted_iota(jnp.int32, (rows, 1), 0) % rpad
        bias_col = bias_ref[:, 0:1]
        neg_suffix = jnp.zeros((rows, 1), F32)
        acc = jnp.zeros((rows, HEAD_DIM), F32)
        for j in reversed(range(ts)):
            valid = row_t > j
            zp = jnp.sum(qf * kn_ref[j], axis=1, keepdims=True) + bias_col
            l, log_sig = _sb_terms(zp)
            wj = jnp.where(valid, jnp.exp2(log_sig + neg_suffix), 0.0)
            neg_suffix = neg_suffix - jnp.where(valid, l, 0.0)
            acc = acc + wj * vn_ref[j]
        acc_ref[...] = acc
        run_ref[...] = jnp.broadcast_to(neg_suffix, run_ref.shape)

    def head_rows(ref, h):
        return ref[pl.ds(h, PAGE_SIZE, stride=heads), :].astype(BF16)

    scores = []
    for h in range(heads):
        qh = qf[h * rpad:(h + 1) * rpad].astype(BF16)
        scores.append(jnp.concatenate([_dot_nt(qh, head_rows(k_ref, h)) for k_ref in k_refs], axis=1))
    zp = jnp.concatenate(scores, axis=0) + bias_ref[...]
    l, log_sig = _sb_terms(zp)
    c = _dot(l.astype(BF16), tri_ref[...])
    run = run_ref[...]
    w = jnp.exp2(log_sig + c + jnp.concatenate([run] * g_pages, axis=1))
    pvs = []
    for h in range(heads):
        vh = jnp.concatenate([head_rows(v_ref, h) for v_ref in v_refs], axis=0)
        pvs.append(_dot(w[h * rpad:(h + 1) * rpad].astype(BF16), vh))
    acc_ref[...] += jnp.concatenate(pvs, axis=0)
    run_ref[...] = run + jnp.broadcast_to(c[:, 0:1] - l[:, 0:1], run.shape)

    @pl.when(s == pl.num_programs(1) - 1)
    def _():
        for h in range(heads):
            o_ref[:, h * HEAD_DIM:(h + 1) * HEAD_DIM] = acc_ref[h * rpad:h * rpad + ts, :]


def _sb_sample(q, k_new, v_new, cache_k, cache_v, page_table, bias):
    db, ts, width = q.shape
    heads = width // HEAD_DIM
    n_pages = page_table.shape[1]
    g_pages = PAGES_PER_STEP
    steps = n_pages // g_pages
    rpad = SB_SAMPLE_ROWS
    rows = heads * rpad
    span = g_pages * PAGE_SIZE

    def head_major(a):
        return a.reshape(db, ts, heads, HEAD_DIM).transpose(0, 2, 1, 3)

    def per_row(a):
        a = a.reshape(db, ts, heads, 1, HEAD_DIM)
        return jnp.broadcast_to(a, (db, ts, heads, rpad, HEAD_DIM)).reshape(db, ts, rows, HEAD_DIM)

    q_rows = jnp.pad(head_major(q), ((0, 0), (0, 0), (0, rpad - ts), (0, 0))).reshape(db, rows, HEAD_DIM)
    bias_rows = jnp.broadcast_to(jnp.repeat(bias * LOG2E, rpad)[:, None], (rows, span))

    def page_spec(g):
        def index(b, s, pt):
            return (pt[b, n_pages - (s + 1) * g_pages + g], 0, 0)
        return pl.BlockSpec((None, PAGE_SIZE * heads, HEAD_DIM), index)

    grid_spec = pltpu.PrefetchScalarGridSpec(
        num_scalar_prefetch=1,
        grid=(db, steps),
        in_specs=[
            pl.BlockSpec((None, rows, HEAD_DIM), lambda b, s, pt: (b, 0, 0)),
            pl.BlockSpec((rows, span), lambda b, s, pt: (0, 0)),
            pl.BlockSpec((None, ts, rows, HEAD_DIM), lambda b, s, pt: (b, 0, 0, 0)),
            pl.BlockSpec((None, ts, rows, HEAD_DIM), lambda b, s, pt: (b, 0, 0, 0)),
            pl.BlockSpec((span, span), lambda b, s, pt: (0, 0)),
        ] + [page_spec(g) for g in range(g_pages)] * 2,
        out_specs=pl.BlockSpec((None, ts, width), lambda b, s, pt: (b, 0, 0)),
        scratch_shapes=[pltpu.VMEM((rows, HEAD_DIM), F32), pltpu.VMEM((rows, HEAD_DIM), F32)],
    )
    return pl.pallas_call(
        _sb_sample_kernel,
        grid_spec=grid_spec,
        out_shape=jax.ShapeDtypeStruct((db, ts, width), F32),
        compiler_params=_params("parallel", "arbitrary"),
        name="sb_sample",
    )(page_table, q_rows, bias_rows, per_row(k_new), per_row(v_new), _neg_strict_suffix_tri(span),
      *([cache_k] * g_pages), *([cache_v] * g_pages))


def _gelu_tanh(x):
    return 0.5 * x * (1.0 + jnp.tanh(math.sqrt(2.0 / math.pi) * (x + 0.044715 * (x * x * x))))


def _lru_gates(xc, wa_ref, ba, wx_ref, bx, lam):
    xg = xc.astype(BF16)
    blocks = [xg[:, n * LRU_BLOCK_W:(n + 1) * LRU_BLOCK_W] for n in range(LRU_BLOCKS)]
    ra = jnp.concatenate([_dot(blocks[n], wa_ref[n]) for n in range(LRU_BLOCKS)], axis=1) + ba
    rx = jnp.concatenate([_dot(blocks[n], wx_ref[n]) for n in range(LRU_BLOCKS)], axis=1) + bx
    r = jax.nn.sigmoid(ra)
    gate_in = jax.nn.sigmoid(rx)
    log_a = -LRU_C * r * _softplus(-lam)
    a = jnp.exp(log_a)
    u = jnp.sqrt(-jnp.tanh(log_a) * (a * a + 1.0)) * gate_in * xc
    return a, u


def _shift_rows(x, s, fill):
    rows = x.shape[0]
    if s % 8 == 0:
        pad = jnp.full((s,) + x.shape[1:], fill, x.dtype)
        return jnp.concatenate([pad, x[:rows - s]], axis=0)
    rolled = pltpu.roll(x, s, 0)
    t = lax.broadcasted_iota(jnp.int32, x.shape, 0)
    return jnp.where(t >= s, rolled, fill)


def _lru_prompt_kernel(xb_ref, gb_ref, cw_ref, cb_ref, wa_ref, ba_ref, wx_ref, bx_ref, lam_ref,
                       y_ref, hlast_ref, xbuf_ref, h_ref):
    i = pl.program_id(0)
    tt = xb_ref.shape[0]
    pad = 8

    @pl.when(i == 0)
    def _():
        xbuf_ref[0:pad, :] = jnp.zeros((pad, xbuf_ref.shape[1]), F32)
        h_ref[...] = jnp.zeros_like(h_ref)

    xbuf_ref[pad:pad + tt, :] = xb_ref[...]
    xc = cb_ref[...]
    for tap in range(CONV_W):
        off = pad - (CONV_W - 1) + tap
        xc = xc + cw_ref[tap:tap + 1, :] * xbuf_ref[off:off + tt, :]
    xbuf_ref[0:pad, :] = xbuf_ref[tt:tt + pad, :]

    a, u = _lru_gates(xc, wa_ref, ba_ref[...], wx_ref, bx_ref[...], lam_ref[...])
    s = 1
    while s < tt:
        u = a * _shift_rows(u, s, 0.0) + u
        a = a * _shift_rows(a, s, 1.0)
        s *= 2
    hs = a * h_ref[...] + u
    h_ref[...] = hs[tt - 1:tt, :]
    y_ref[...] = hs * _gelu_tanh(gb_ref[...])

    @pl.when(i == pl.num_programs(0) - 1)
    def _():
        hlast_ref[...] = hs[tt - 1:tt, :]


def _lru_prompt(xb, gb, conv_w, conv_b, wa, ba, wx, bx, lam):
    t, width = xb.shape
    tt = LRU_TT
    row = pl.BlockSpec((tt, width), lambda i: (i, 0))
    vec = pl.BlockSpec((1, width), lambda i: (0, 0))
    gate = pl.BlockSpec((LRU_BLOCKS, LRU_BLOCK_W, LRU_BLOCK_W), lambda i: (0, 0, 0))
    return pl.pallas_call(
        _lru_prompt_kernel,
        grid=(t // tt,),
        in_specs=[row, row, pl.BlockSpec((CONV_W, width), lambda i: (0, 0)), vec, gate, vec, gate, vec, vec],
        out_specs=[row, vec],
        out_shape=[jax.ShapeDtypeStruct((t, width), F32), jax.ShapeDtypeStruct((1, width), F32)],
        scratch_shapes=[pltpu.VMEM((tt + 8, width), F32), pltpu.VMEM((1, width), F32)],
        compiler_params=_params("arbitrary"),
        name="lru_prompt",
    )(xb, gb, conv_w, conv_b, wa, ba, wx, bx, lam)


def _lru_sample_kernel(xb_ref, gb_ref, sc_ref, h0_ref, cw_ref, cb_ref, wa_ref, ba_ref, wx_ref, bx_ref, lam_ref,
                       y_ref, hlast_ref):
    ts = xb_ref.shape[0]
    xp = [sc_ref[j] for j in range(CONV_W - 1)] + [xb_ref[t] for t in range(ts)]
    xcs = []
    for t in range(ts):
        xc = cb_ref[...]
        for tap in range(CONV_W):
            xc = xc + cw_ref[tap:tap + 1, :] * xp[t + tap]
        xcs.append(xc)
    db = xcs[0].shape[0]
    a, u = _lru_gates(jnp.concatenate(xcs, axis=0), wa_ref, ba_ref[...], wx_ref, bx_ref[...], lam_ref[...])
    h = h0_ref[...]
    for t in range(ts):
        h = a[t * db:(t + 1) * db] * h + u[t * db:(t + 1) * db]
        y_ref[t] = h * _gelu_tanh(gb_ref[t])
    hlast_ref[...] = h


def _lru_sample(xb, gb, state_conv, h0, conv_w, conv_b, wa, ba, wx, bx, lam):
    ts, db, width = xb.shape
    return pl.pallas_call(
        _lru_sample_kernel,
        out_shape=[jax.ShapeDtypeStruct((ts, db, width), F32), jax.ShapeDtypeStruct((db, width), F32)],
        compiler_params=pltpu.CompilerParams(vmem_limit_bytes=VMEM_LIMIT_BYTES),
        name="lru_sample",
    )(xb, gb, state_conv, h0, conv_w, conv_b, wa, ba, wx, bx, lam)


def _head_rms(x, gain):
    heads = x.shape[1] // HEAD_DIM
    return jnp.concatenate(
        [_rms(x[:, h * HEAD_DIM:(h + 1) * HEAD_DIM], gain) for h in range(heads)], axis=1)


def _mem_kv_kernel(mem_ref, g_ref, wk_ref, wv_ref, kg_ref, k_ref, v_ref):
    m = _rms(mem_ref[...], g_ref[...]).astype(BF16)
    k_ref[...] = _head_rms(_dot(m, wk_ref[...]), kg_ref[...])
    v_ref[...] = _dot(m, wv_ref[...])


def _mem_kv(mem, g, wk, wv, k_gain):
    n = mem.shape[0]
    out = jax.ShapeDtypeStruct((n, wk.shape[1]), F32)
    return pl.pallas_call(
        _mem_kv_kernel,
        out_shape=[out, out],
        compiler_params=pltpu.CompilerParams(vmem_limit_bytes=VMEM_LIMIT_BYTES),
        name="mem_kv",
    )(mem, g, wk, wv, k_gain)


def _outproj_kernel(sb_ref, lru_ref, x_ref, wo_ref, g_ref, wq_ref, qg_ref, h_ref, qn_ref):
    half = sb_ref.shape[1]
    h = (x_ref[...] + _dot(sb_ref[...].astype(BF16), wo_ref[0:half, :])
         + _dot(lru_ref[...].astype(BF16), wo_ref[half:2 * half, :]))
    h_ref[...] = h
    q = _dot(_rms(h, g_ref[...]).astype(BF16), wq_ref[...])
    qn_ref[...] = _head_rms(q, qg_ref[...])


def _outproj(sb, lru, x, wo, g, wq, q_gain, tm):
    rows, d = x.shape
    half = sb.shape[1]
    mw = wq.shape[1]
    const = lambda i: (0, 0)
    return pl.pallas_call(
        _outproj_kernel,
        grid=(rows // tm,),
        in_specs=[
            pl.BlockSpec((tm, half), lambda i: (i, 0)),
            pl.BlockSpec((tm, half), lambda i: (i, 0)),
            pl.BlockSpec((tm, d), lambda i: (i, 0)),
            pl.BlockSpec(wo.shape, const),
            pl.BlockSpec((1, d), const),
            pl.BlockSpec(wq.shape, const),
            pl.BlockSpec((1, HEAD_DIM), const),
        ],
        out_specs=[pl.BlockSpec((tm, d), lambda i: (i, 0)), pl.BlockSpec((tm, mw), lambda i: (i, 0))],
        out_shape=[jax.ShapeDtypeStruct((rows, d), F32), jax.ShapeDtypeStruct((rows, mw), F32)],
        compiler_params=_params("parallel"),
        name="outproj",
    )(sb, lru, x, wo, g, wq, q_gain)


def _mem_attn_kernel(q_ref, k_ref, v_ref, o_ref):
    q = q_ref[...]
    k = k_ref[...]
    v = v_ref[...]
    outs = []
    for h in range(MEM_HEADS):
        sl = slice(h * HEAD_DIM, (h + 1) * HEAD_DIM)
        s = _dot_nt(q[:, sl].astype(BF16), k[:, sl].astype(BF16)) * SCALE
        e = jnp.exp(s - jnp.max(s, axis=-1, keepdims=True))
        denom = jnp.sum(e, axis=-1, keepdims=True)
        outs.append(_dot(e.astype(BF16), v[:, sl].astype(BF16)) / denom)
    o_ref[...] = jnp.concatenate(outs, axis=1)


def _mem_attn(q, mk, mv, tq):
    b, t, mw = q.shape
    n = mk.shape[1]
    q_spec = pl.BlockSpec((None, tq, mw), lambda bi, i: (bi, i, 0))
    kv_spec = pl.BlockSpec((None, n, mw), lambda bi, i: (bi, 0, 0))
    return pl.pallas_call(
        _mem_attn_kernel,
        grid=(b, t // tq),
        in_specs=[q_spec, kv_spec, kv_spec],
        out_specs=q_spec,
        out_shape=jax.ShapeDtypeStruct((b, t, mw), F32),
        compiler_params=_params("parallel", "parallel"),
        name="mem_attn",
    )(q, mk, mv)


def _ffn_kernel(h_ref, o_ref, wmo_ref, g_ref, wup_ref, wdn_ref, y_ref, hn_ref):
    f = pl.program_id(1)

    @pl.when(f == 0)
    def _():
        h2 = h_ref[...] + _dot(o_ref[...].astype(BF16), wmo_ref[...])
        y_ref[...] = h2
        hn_ref[...] = _rms(h2, g_ref[...]).astype(BF16)

    up = _dot(hn_ref[...], wup_ref[...])
    act = jnp.square(jnp.maximum(up, 0.0))
    y_ref[...] += _dot(act.astype(BF16), wdn_ref[...])


def _ffn(h, o, wmo, g, wup, wdn, tm, tf):
    rows, d = h.shape
    mw = o.shape[1]
    dff = wup.shape[1]
    return pl.pallas_call(
        _ffn_kernel,
        grid=(rows // tm, dff // tf),
        in_specs=[
            pl.BlockSpec((tm, d), lambda i, f: (i, 0)),
            pl.BlockSpec((tm, mw), lambda i, f: (i, 0)),
            pl.BlockSpec((mw, d), lambda i, f: (0, 0)),
            pl.BlockSpec((1, d), lambda i, f: (0, 0)),
            pl.BlockSpec((d, tf), lambda i, f: (0, f)),
            pl.BlockSpec((tf, d), lambda i, f: (f, 0)),
        ],
        out_specs=pl.BlockSpec((tm, d), lambda i, f: (i, 0)),
        out_shape=jax.ShapeDtypeStruct((rows, d), F32),
        scratch_shapes=[pltpu.VMEM((tm, d), BF16)],
        compiler_params=_params("parallel", "arbitrary"),
        name="ffn",
    )(h, o, wmo, g, wup, wdn)


def kernel(x_prompt, x_sample, mem_prompt, cache_sb_k, cache_sb_v, page_table, state_conv, state_h, cache_mem_k, cache_mem_v, norm_mix, w_in, sb_bias, conv_w, conv_b, gate_a_w, gate_a_b, gate_x_w, gate_x_b, lru_lambda, w_out, norm_mem_q, norm_mem_kv, w_mq, w_mk, w_mv, mq_gain, mk_gain, w_mo, norm_ffn, w_up, w_down):
    depth = w_in.shape[0]
    assert depth == 1 and x_prompt.shape[0] == 1
    bp, seq, d = x_prompt.shape
    db, ts, _ = x_sample.shape
    n_mem = mem_prompt.shape[1]
    l = 0

    row = lambda a: a[l].reshape(1, -1)
    w_in_b = w_in[l].astype(BF16)
    w_out_b = w_out[l].astype(BF16)
    w_mq_b = w_mq[l].astype(BF16)
    w_mk_b = w_mk[l].astype(BF16)
    w_mv_b = w_mv[l].astype(BF16)
    w_mo_b = w_mo[l].astype(BF16)
    w_up_b = w_up[l].astype(BF16)
    w_dn_b = w_down[l].astype(BF16)
    wa_b = gate_a_w[l].astype(BF16)
    wx_b = gate_x_w[l].astype(BF16)
    lru_vecs = (conv_w[l], row(conv_b), wa_b, row(gate_a_b), wx_b, row(gate_x_b), row(lru_lambda))
    bias = sb_bias[l]
    sb_w = w_in.shape[2] // 5
    heads = sb_w // HEAD_DIM
    mem_w = w_mq.shape[2]

    xp = x_prompt.reshape(seq, d)
    q, k_p, v_p, xb_p, gb, kb, vb = _inproj(xp, row(norm_mix), w_in_b, tm=512)
    sb = _sb_prompt(q, kb, vb, bias)
    lru, h_p = _lru_prompt(xb_p, gb, *lru_vecs)
    mk_p, mv_p = _mem_kv(mem_prompt.reshape(n_mem, d), row(norm_mem_kv), w_mk_b, w_mv_b, row(mk_gain))
    h, qn = _outproj(sb, lru, xp, w_out_b, row(norm_mem_q), w_mq_b, row(mq_gain), tm=256)
    o = _mem_attn(qn.reshape(1, seq, mem_w), mk_p.reshape(1, n_mem, mem_w), mv_p.reshape(1, n_mem, mem_w), tq=512)
    y_p = _ffn(h, o.reshape(seq, mem_w), w_mo_b, row(norm_ffn), w_up_b, w_dn_b, tm=512, tf=1024)

    rows_s = db * ts
    xs = x_sample.reshape(rows_s, d)
    q, k_s, v_s, xb_s, gb, _, _ = _inproj(xs, row(norm_mix), w_in_b, tm=rows_s)
    n_pool = cache_sb_k.shape[1]
    sb = _sb_sample(q.reshape(db, ts, sb_w), k_s.reshape(db, ts, sb_w), v_s.reshape(db, ts, sb_w),
                    cache_sb_k[l].reshape(n_pool, PAGE_SIZE * heads, HEAD_DIM),
                    cache_sb_v[l].reshape(n_pool, PAGE_SIZE * heads, HEAD_DIM), page_table, bias)
    time_major = lambda a: a.reshape(db, ts, -1).transpose(1, 0, 2)
    lru_t, h_s = _lru_sample(time_major(xb_s), time_major(gb), state_conv[l].transpose(1, 0, 2), state_h[l], *lru_vecs)
    lru = lru_t.transpose(1, 0, 2).reshape(rows_s, -1)
    h, qn = _outproj(sb.reshape(rows_s, sb_w), lru, xs, w_out_b, row(norm_mem_q), w_mq_b, row(mq_gain), tm=rows_s)
    tpad = 8
    qn_pad = jnp.pad(qn.reshape(db, ts, mem_w), ((0, 0), (0, tpad - ts), (0, 0)))
    o = _mem_attn(qn_pad, cache_mem_k[l].reshape(db, n_mem, mem_w), cache_mem_v[l].reshape(db, n_mem, mem_w), tq=tpad)
    o = o[:, :ts].reshape(rows_s, mem_w)
    y_s = _ffn(h, o, w_mo_b, row(norm_ffn), w_up_b, w_dn_b, tm=rows_s, tf=1024)

    kv_p = lambda a: a.reshape(1, bp, seq, heads, HEAD_DIM)
    kv_s = lambda a: a.reshape(1, db, ts, heads, HEAD_DIM)
    conv_p = xb_p[seq - (CONV_W - 1):].reshape(1, bp, CONV_W - 1, -1)
    conv_s = xb_s.reshape(db, ts, -1)[:, ts - (CONV_W - 1):].reshape(1, db, CONV_W - 1, -1)
    return (y_p.reshape(bp, seq, d), y_s.reshape(db, ts, d),
            kv_p(k_p), kv_p(v_p), kv_s(k_s), kv_s(v_s),
            conv_p, h_p.reshape(1, bp, -1), conv_s, h_s.reshape(1, db, -1),
            mk_p.reshape(1, bp, n_mem, MEM_HEADS, HEAD_DIM), mv_p.reshape(1, bp, n_mem, MEM_HEADS, HEAD_DIM))
```
